```python
import math
import jax, jax.numpy as jnp
from jax import lax
import numpy as np

D_MODEL = 1024
BATCH = 8
SEQ = 4096
DEPTH = 2

N_EVEN = (DEPTH + 1) // 2
N_ODD = DEPTH // 2
RMS_EPS = 1e-6
D_FF = 2816
MAX_POS_OFFSET = 1024

GLA_HEADS = 4
GLA_DK = 128
GLA_DV = 256
GLA_QK = GLA_HEADS * GLA_DK
GLA_V = GLA_HEADS * GLA_DV
GLA_GATE_RANK = 16
GLA_GATE_NORMALIZER = 16.0
GLA_CHUNK = 64

SSM_HEADS = 16
SSM_HEADDIM = 64
SSM_DINNER = SSM_HEADS * SSM_HEADDIM
SSM_GROUPS = 4
SSM_DSTATE = 128
SSM_CONV = 4
SSM_CHUNK = 64
SSM_CONV_DIM = SSM_DINNER + 2 * SSM_GROUPS * SSM_DSTATE

AB_SPLITS = (GLA_QK, GLA_QK, GLA_V, GLA_V, GLA_GATE_RANK, SSM_DINNER, SSM_CONV_DIM, SSM_HEADS)
AB_IN = sum(AB_SPLITS)
AB_CAT = GLA_V + SSM_DINNER

MLA_HEADS = 8
MLA_Q_RANK = 384
MLA_KV_RANK = 256
MLA_NOPE = 128
MLA_ROPE = 64
MLA_V = 128
MLA_QK = MLA_NOPE + MLA_ROPE
MLA_DOWN = MLA_Q_RANK + MLA_KV_RANK + MLA_ROPE
ROPE_THETA = 10000.0
ATTN_BLOCK = 128

kernel_name = "hybrid_gla_ssd_mla_macaron"


def rmsnorm(x, w):
    xf = x.astype(jnp.float32)
    y = xf * lax.rsqrt(jnp.mean(xf * xf, axis=-1, keepdims=True) + RMS_EPS)
    return (y * w.astype(jnp.float32)).astype(x.dtype)


def swiglu(h, w_gate, w_up, w_down):
    return (jax.nn.silu(h @ w_gate) * (h @ w_up)) @ w_down


def gla_chunked(q, k, v, log_a):
    b_, t, h, dk = q.shape
    dv = v.shape[-1]
    nc = t // GLA_CHUNK
    f32 = jnp.float32

    def chunks(u):
        return u.astype(f32).reshape(b_, nc, GLA_CHUNK, h, u.shape[-1]).transpose(1, 0, 3, 2, 4)

    qc, kc, vc = chunks(q), chunks(k), chunks(v)
    bc = jnp.cumsum(chunks(log_a), axis=-2)
    causal = jnp.tril(jnp.ones((GLA_CHUNK, GLA_CHUNK), dtype=bool))[:, :, None]

    def step(state, inp):
        qi, ki, vi, bi = inp
        b_last = bi[:, :, -1:, :]
        o_inter = jnp.einsum('bhik,bhkv->bhiv', qi * jnp.exp(bi), state)
        decay = jnp.exp(jnp.where(causal, bi[:, :, :, None, :] - bi[:, :, None, :, :], -jnp.inf))
        scores = jnp.einsum('bhik,bhijk->bhij', qi, decay * ki[:, :, None, :, :])
        o = o_inter + jnp.einsum('bhij,bhjv->bhiv', scores, vi)
        k_dec = ki * jnp.exp(b_last - bi)
        state = state * jnp.exp(b_last)[:, :, 0, :, None] + jnp.einsum('bhjk,bhjv->bhkv', k_dec, vi)
        return state, o

    s0 = jnp.zeros((b_, h, dk, dv), f32)
    _, out = lax.scan(step, s0, (qc, kc, vc, bc))
    return out.transpose(1, 0, 3, 2, 4).reshape(b_, t, h, dv).astype(v.dtype)


def segsum(a):
    n = a.shape[-1]
    cs = jnp.cumsum(a, axis=-1)
    mask = jnp.tril(jnp.ones((n, n), dtype=bool))
    return jnp.where(mask, cs[..., :, None] - cs[..., None, :], -jnp.inf)


def ssd_chunked(x, a, bm, cm):
    b_, t, h, p = x.shape
    g, n = bm.shape[-2:]
    r = h // g
    c = t // SSM_CHUNK
    l = SSM_CHUNK
    x = x.reshape(b_, c, l, g, r, p)
    a = a.reshape(b_, c, l, g, r).transpose(0, 3, 4, 1, 2)
    bm = bm.reshape(b_, c, l, g, n)
    cm = cm.reshape(b_, c, l, g, n)
    a_cum = jnp.cumsum(a, axis=-1)
    decay_in = jnp.exp(segsum(a))
    cb = jnp.einsum('bclgn,bcsgn->bgcls', cm, bm)
    y_diag = jnp.einsum('bgrcls,bcsgrp->bclgrp', cb[:, :, None] * decay_in, x)
    decay_states = jnp.exp(a_cum[..., -1:] - a_cum).transpose(0, 3, 4, 1, 2)
    states = jnp.einsum('bcsgn,bcsgrp->bcgrpn', bm, x * decay_states[..., None])
    states = jnp.concatenate([jnp.zeros_like(states[:, :1]), states], axis=1)
    chunk_decay = jnp.exp(segsum(jnp.pad(a_cum[..., -1], ((0, 0), (0, 0), (0, 0), (1, 0)))))
    states = jnp.einsum('bgrzc,bcgrpn->bzgrpn', chunk_decay, states)[:, :-1]
    y_off = jnp.einsum('bclgn,bcgrpn->bclgrp', cm, states) * jnp.exp(a_cum).transpose(0, 3, 4, 1, 2)[..., None]
    return (y_diag + y_off).reshape(b_, t, h, p)


def causal_depthwise_conv(u, w, bias):
    ch = u.shape[-1]
    out = lax.conv_general_dilated(u, w[:, None, :].astype(u.dtype), window_strides=(1,),
                                   padding=[(SSM_CONV - 1, 0)], dimension_numbers=('NWC', 'WIO', 'NWC'),
                                   feature_group_count=ch)
    return out + bias.astype(u.dtype)


def gla_ssd_mixer(h, w_in, gla_w_gate2, gla_b_gate2, gla_norm_w, conv_w, conv_b, dt_bias, a_log,
                  d_skip, ssm_norm_w, w_out):
    f32 = jnp.float32
    b_, t, _ = h.shape
    points, acc = [], 0
    for s in AB_SPLITS[:-1]:
        acc += s
        points.append(acc)
    q, k, v, g, a_low, z, xbc, dt = jnp.split(h @ w_in, points, axis=-1)
    q = q.reshape(b_, t, GLA_HEADS, GLA_DK) * (GLA_DK ** -0.5)
    k = k.reshape(b_, t, GLA_HEADS, GLA_DK)
    v = v.reshape(b_, t, GLA_HEADS, GLA_DV)
    log_a = jax.nn.log_sigmoid((a_low @ gla_w_gate2 + gla_b_gate2).astype(f32)) / GLA_GATE_NORMALIZER
    o = gla_chunked(q, k, v, log_a.reshape(b_, t, GLA_HEADS, GLA_DK))
    o = rmsnorm(o, gla_norm_w) * jax.nn.silu(g.reshape(b_, t, GLA_HEADS, GLA_DV))
    o_gla = o.reshape(b_, t, GLA_V)
    xbc = jax.nn.silu(causal_depthwise_conv(xbc, conv_w, conv_b))
    xs, bm, cm = jnp.split(xbc, [SSM_DINNER, SSM_DINNER + SSM_GROUPS * SSM_DSTATE], axis=-1)
    dt = jax.nn.softplus(dt.astype(f32) + dt_bias.astype(f32))
    a = -jnp.exp(a_log.astype(f32))
    xs = xs.reshape(b_, t, SSM_HEADS, SSM_HEADDIM).astype(f32)
    y = ssd_chunked(xs * dt[..., None], dt * a,
                    bm.reshape(b_, t, SSM_GROUPS, SSM_DSTATE).astype(f32),
                    cm.reshape(b_, t, SSM_GROUPS, SSM_DSTATE).astype(f32))
    y = y + d_skip.astype(f32)[:, None] * xs
    y = y.reshape(b_, t, SSM_DINNER) * jax.nn.silu(z.astype(f32))
    gs = SSM_DINNER // SSM_GROUPS
    y = rmsnorm(y.reshape(b_, t, SSM_GROUPS, gs), ssm_norm_w.reshape(SSM_GROUPS, gs))
    y = y.reshape(b_, t, SSM_DINNER).astype(h.dtype)
    return jnp.concatenate([o_gla, y], axis=-1) @ w_out


def apply_rope(u, cos, sin):
    u1, u2 = jnp.split(u, 2, axis=-1)
    return jnp.concatenate([u1 * cos - u2 * sin, u2 * cos + u1 * sin], axis=-1).astype(u.dtype)


def mla_mixer(h, positions, w_down, q_norm_w, w_uq, kv_norm_w, w_ukv, w_o):
    f32 = jnp.float32
    b_, t, _ = h.shape
    cq, ckv, k_rope = jnp.split(h @ w_down, [MLA_Q_RANK, MLA_Q_RANK + MLA_KV_RANK], axis=-1)
    q = (rmsnorm(cq, q_norm_w) @ w_uq).reshape(b_, t, MLA_HEADS, MLA_QK)
    q_nope, q_rope = q[..., :MLA_NOPE], q[..., MLA_NOPE:]
    kv = (rmsnorm(ckv, kv_norm_w) @ w_ukv).reshape(b_, t, MLA_HEADS, MLA_NOPE + MLA_V)
    k_nope, v = kv[..., :MLA_NOPE], kv[..., MLA_NOPE:]
    inv_freq = ROPE_THETA ** (-jnp.arange(0, MLA_ROPE, 2, dtype=f32) / MLA_ROPE)
    ang = positions.astype(f32)[..., None] * inv_freq
    cos, sin = jnp.cos(ang), jnp.sin(ang)
    q_rope = apply_rope(q_rope, cos[:, :, None, :], sin[:, :, None, :])
    k_rope = apply_rope(k_rope, cos, sin)
    scale = MLA_QK ** -0.5
    outs = []
    for start in range(0, t, ATTN_BLOCK):
        end = start + ATTN_BLOCK
        s = (jnp.einsum('bqhd,bkhd->bhqk', q_nope[:, start:end], k_nope[:, :end])
             + jnp.einsum('bqhd,bkd->bhqk', q_rope[:, start:end], k_rope[:, :end])).astype(f32) * scale
        mask = (start + jnp.arange(ATTN_BLOCK))[:, None] >= jnp.arange(end)[None, :]
        p = jax.nn.softmax(jnp.where(mask, s, -jnp.inf), axis=-1).astype(v.dtype)
        outs.append(jnp.einsum('bhqk,bkhd->bqhd', p, v[:, :end]))
    o = jnp.concatenate(outs, axis=1).reshape(b_, t, MLA_HEADS * MLA_V)
    return o @ w_o


def setup_inputs(seed: int = 0) -> dict:
    key = jax.random.key(seed)
    keys = list(jax.random.split(key, 64))
    f32 = jnp.float32

    def nk():
        return keys.pop()

    def w(shape, fan_in):
        return jax.random.normal(nk(), shape, f32) * (fan_in ** -0.5)

    def gain(shape):
        return 1.0 + 0.05 * jax.random.normal(nk(), shape, f32)

    def bias(shape, s=0.02):
        return s * jax.random.normal(nk(), shape, f32)

    d, fdim = D_MODEL, D_FF
    x = jax.random.normal(nk(), (BATCH, SEQ, D_MODEL), f32)
    positions = (jax.random.randint(nk(), (BATCH, 1), 0, MAX_POS_OFFSET, dtype=jnp.int32)
                 + jnp.arange(SEQ, dtype=jnp.int32)[None, :])
    dt0 = jnp.exp(jax.random.uniform(nk(), (N_EVEN, SSM_HEADS), f32, math.log(1e-3), math.log(1e-1)))
    dt_bias = dt0 + jnp.log(-jnp.expm1(-dt0))
    a_log = jnp.log(jax.random.uniform(nk(), (N_EVEN, SSM_HEADS), f32, 1.0, 16.0))
    return {
        "x": x,
        "positions": positions,
        "ffn1_pre_norm": gain((DEPTH, d)),
        "ffn1_w_gate": w((DEPTH, d, fdim), d),
        "ffn1_w_up": w((DEPTH, d, fdim), d),
        "ffn1_w_down": w((DEPTH, fdim, d), fdim),
        "ffn1_post_norm": gain((DEPTH, d)),
        "mix_pre_norm": gain((DEPTH, d)),
        "mix_post_norm": gain((DEPTH, d)),
        "ffn2_pre_norm": gain((DEPTH, d)),
        "ffn2_w_gate": w((DEPTH, d, fdim), d),
        "ffn2_w_up": w((DEPTH, d, fdim), d),
        "ffn2_w_down": w((DEPTH, fdim, d), fdim),
        "ffn2_post_norm": gain((DEPTH, d)),
        "ab_w_in": w((N_EVEN, d, AB_IN), d),
        "gla_w_gate2": w((N_EVEN, GLA_GATE_RANK, GLA_QK), GLA_GATE_RANK),
        "gla_b_gate2": bias((N_EVEN, GLA_QK), 0.1),
        "gla_norm": gain((N_EVEN, GLA_DV)),
        "ssm_conv_w": w((N_EVEN, SSM_CONV, SSM_CONV_DIM), SSM_CONV),
        "ssm_conv_b": bias((N_EVEN, SSM_CONV_DIM)),
        "ssm_dt_bias": dt_bias,
        "ssm_a_log": a_log,
        "ssm_d": 1.0 + 0.1 * jax.random.normal(nk(), (N_EVEN, SSM_HEADS), f32),
        "ssm_norm": gain((N_EVEN, SSM_DINNER)),
        "ab_w_out": w((N_EVEN, AB_CAT, d), AB_CAT),
        "mla_w_down": w((N_ODD, d, MLA_DOWN), d),
        "mla_q_norm": gain((N_ODD, MLA_Q_RANK)),
        "mla_w_uq": w((N_ODD, MLA_Q_RANK, MLA_HEADS * MLA_QK), MLA_Q_RANK),
        "mla_kv_norm": gain((N_ODD, MLA_KV_RANK)),
        "mla_w_ukv": w((N_ODD, MLA_KV_RANK, MLA_HEADS * (MLA_NOPE + MLA_V)), MLA_KV_RANK),
        "mla_w_o": w((N_ODD, MLA_HEADS * MLA_V, d), MLA_HEADS * MLA_V),
    }


def reference(x, positions, ffn1_pre_norm, ffn1_w_gate, ffn1_w_up, ffn1_w_down, ffn1_post_norm,
              mix_pre_norm, mix_post_norm, ffn2_pre_norm, ffn2_w_gate, ffn2_w_up, ffn2_w_down,
              ffn2_post_norm, ab_w_in, gla_w_gate2, gla_b_gate2, gla_norm, ssm_conv_w, ssm_conv_b,
              ssm_dt_bias, ssm_a_log, ssm_d, ssm_norm, ab_w_out, mla_w_down, mla_q_norm, mla_w_uq,
              mla_kv_norm, mla_w_ukv, mla_w_o):
    for i in range(DEPTH):
        f = swiglu(rmsnorm(x, ffn1_pre_norm[i]), ffn1_w_gate[i], ffn1_w_up[i], ffn1_w_down[i])
        x = x + 0.5 * rmsnorm(f, ffn1_post_norm[i])
        h = rmsnorm(x, mix_pre_norm[i])
        j = i // 2
        if i % 2 == 0:
            m = gla_ssd_mixer(h, ab_w_in[j], gla_w_gate2[j], gla_b_gate2[j], gla_norm[j], ssm_conv_w[j],
                              ssm_conv_b[j], ssm_dt_bias[j], ssm_a_log[j], ssm_d[j], ssm_norm[j], ab_w_out[j])
        else:
            m = mla_mixer(h, positions, mla_w_down[j], mla_q_norm[j], mla_w_uq[j], mla_kv_norm[j],
                          mla_w_ukv[j], mla_w_o[j])
        x = x + rmsnorm(m, mix_post_norm[i])
        f = swiglu(rmsnorm(x, ffn2_pre_norm[i]), ffn2_w_gate[i], ffn2_w_up[i], ffn2_w_down[i])
        x = x + 0.5 * rmsnorm(f, ffn2_post_norm[i])
    return x
```

```python
import functools
import math

import jax
import jax.numpy as jnp
from jax import lax
from jax.experimental import pallas as pl
from jax.experimental.pallas import tpu as pltpu

F32 = jnp.float32
BF16 = jnp.bfloat16
HIGHEST = lax.Precision.HIGHEST

RMS_EPS = 1e-6
LANES = 128
VMEM_LIMIT_BYTES = 56 * 1024 * 1024

GLA_HEADS, GLA_DK, GLA_DV = 4, 128, 256
GLA_GATE_RANK = 16
GLA_GATE_NORMALIZER = 16.0
GLA_CHUNK = 64
GLA_SUB = 16

SSM_HEADS, SSM_HEADDIM, SSM_GROUPS, SSM_DSTATE, SSM_CONV = 16, 64, 4, 128, 4
SSM_DINNER = SSM_HEADS * SSM_HEADDIM
SSM_BC = SSM_GROUPS * SSM_DSTATE
SSM_CHUNK = 64
SSM_HEADS_PER_GROUP = SSM_HEADS // SSM_GROUPS
SSM_GROUP_WIDTH = SSM_HEADS_PER_GROUP * SSM_HEADDIM

MLA_HEADS, MLA_Q_RANK, MLA_KV_RANK = 8, 384, 256
MLA_NOPE, MLA_ROPE, MLA_V = 128, 64, 128
MLA_QK = MLA_NOPE + MLA_ROPE
MLA_HEAD_PAD = 256
ROPE_THETA = 10000.0


def _cparams(*sem):
    return pltpu.CompilerParams(dimension_semantics=sem, vmem_limit_bytes=VMEM_LIMIT_BYTES)


def _rms(x, w):
    ms = jnp.mean(x * x, axis=-1, keepdims=True)
    return x * lax.rsqrt(ms + RMS_EPS) * w


def _silu(x):
    return x * jax.nn.sigmoid(x)


def _softplus(x):
    return jnp.maximum(x, 0.0) + jnp.log1p(jnp.exp(-jnp.abs(x)))


def _dot(a, b):
    return jnp.dot(a, b, preferred_element_type=F32)


def _dot_nt(a, b):
    return lax.dot_general(a, b, (((1,), (1,)), ((), ())), preferred_element_type=F32)


def _dot_tn(a, b):
    return lax.dot_general(a, b, (((0,), (0,)), ((), ())), preferred_element_type=F32)


def _const_spec(shape):
    nd = len(shape)
    return pl.BlockSpec(shape, lambda *_: (0,) * nd, pipeline_mode=pl.Buffered(1))


def _ffn_kernel(x_ref, pre_ref, wgu_ref, wd_ref, post_ref, o_ref, acc_ref, *, n_chunks, fc):
    x = x_ref[...]
    h = _rms(x, pre_ref[...]).astype(BF16)

    def body(c, carry):
        gu = _dot(h, wgu_ref[c])
        a = (_silu(gu[:, :fc]) * gu[:, fc:]).astype(BF16)
        acc_ref[...] += _dot(a, wd_ref[c])
        return carry

    acc_ref[...] = jnp.zeros_like(acc_ref)
    lax.fori_loop(0, n_chunks, body, 0)
    o_ref[...] = x + 0.5 * _rms(acc_ref[...], post_ref[...])


def _ffn(x, pre, wgu, wd, post, *, tm):
    n, d = x.shape
    n_chunks, _, fc2 = wgu.shape
    fc = fc2 // 2
    return pl.pallas_call(
        functools.partial(_ffn_kernel, n_chunks=n_chunks, fc=fc),
        grid=(n // tm,),
        in_specs=[
            pl.BlockSpec((tm, d), lambda i: (i, 0)),
            _const_spec((1, d)),
            _const_spec(wgu.shape),
            _const_spec(wd.shape),
            _const_spec((1, d)),
        ],
        out_specs=pl.BlockSpec((tm, d), lambda i: (i, 0)),
        out_shape=jax.ShapeDtypeStruct((n, d), F32),
        scratch_shapes=[pltpu.VMEM((tm, d), F32)],
        compiler_params=_cparams("parallel"),
        name="ffn",
    )(x, pre, wgu, wd, post)


def _prep_ffn(w_gate, w_up, w_down, fc):
    d, f = w_gate.shape
    nc = f // fc
    wg = w_gate.reshape(d, nc, fc).transpose(1, 0, 2)
    wu = w_up.reshape(d, nc, fc).transpose(1, 0, 2)
    wgu = jnp.concatenate([wg, wu], axis=-1).astype(BF16)
    wd = w_down.reshape(nc, fc, d).astype(BF16)
    return wgu, wd


def _mix_out_kernel(*refs, n_parts):
    x_ref = refs[0]
    parts = refs[1:1 + n_parts]
    ws = refs[1 + n_parts:1 + 2 * n_parts]
    post_ref, o_ref = refs[1 + 2 * n_parts:]
    m = _dot(parts[0][...], ws[0][...])
    for p_ref, w_ref in zip(parts[1:], ws[1:]):
        m = m + _dot(p_ref[...], w_ref[...])
    o_ref[...] = x_ref[...] + _rms(m, post_ref[...])


def _mix_out(x, parts, ws, post, *, tm):
    n, d = x.shape
    tok = lambda w: pl.BlockSpec((tm, w), lambda i: (i, 0))
    return pl.pallas_call(
        functools.partial(_mix_out_kernel, n_parts=len(parts)),
        grid=(n // tm,),
        in_specs=[tok(d)] + [tok(p.shape[1]) for p in parts] + [_const_spec(w.shape) for w in ws]
        + [_const_spec((1, d))],
        out_specs=tok(d),
        out_shape=jax.ShapeDtypeStruct((n, d), F32),
        compiler_params=_cparams("parallel"),
        name="mix_out",
    )(x, *parts, *ws, post)


AB_BIG = (("q", 512), ("k", 512), ("v", 1024), ("g", 1024), ("z", 1024), ("xbc", 2048))


def _ab_in_kernel(x_ref, pre_ref, wbig_ref, wsm_ref, q_ref, k_ref, v_ref, g_ref, z_ref, xbc_ref, sm_ref):
    h = _rms(x_ref[...], pre_ref[...]).astype(BF16)
    off = 0
    for ref in (q_ref, k_ref, v_ref, g_ref, z_ref, xbc_ref):
        w = ref.shape[-1]
        ref[...] = _dot(h, wbig_ref[:, off:off + w]).astype(ref.dtype)
        off += w
    sm_ref[...] = _dot(h, wsm_ref[...])


def _ab_in(x, pre, wbig, wsm, *, tm):
    n, d = x.shape
    tok = lambda w: pl.BlockSpec((tm, w), lambda i: (i, 0))
    out_shapes = [jax.ShapeDtypeStruct((n, w), BF16) for _, w in AB_BIG]
    out_shapes += [jax.ShapeDtypeStruct((n, LANES), F32)]
    out_specs = [tok(w) for _, w in AB_BIG] + [tok(LANES)]
    return pl.pallas_call(
        _ab_in_kernel,
        grid=(n // tm,),
        in_specs=[tok(d), _const_spec((1, d)), _const_spec(wbig.shape), _const_spec(wsm.shape)],
        out_specs=out_specs,
        out_shape=out_shapes,
        compiler_params=_cparams("parallel"),
        name="ab_in",
    )(x, pre, wbig, wsm)


def _gla_kernel(q_ref, k_ref, v_ref, g_ref, sm_ref, w2_ref, b2_ref, nw_ref, wsel_ref,
                o_ref, st_ref, la_ref, *, n_chunks):
    c_len, sub = GLA_CHUNK, GLA_SUB
    n_sub = c_len // sub

    @pl.when(pl.program_id(2) == 0)
    def _():
        st_ref[...] = jnp.zeros_like(st_ref)

    xg = jnp.dot(sm_ref[...], w2_ref[0], preferred_element_type=F32, precision=HIGHEST) + b2_ref[0]
    la_ref[...] = (jnp.minimum(xg, 0.0) - jnp.log1p(jnp.exp(-jnp.abs(xg)))) * (1.0 / GLA_GATE_NORMALIZER)

    row = lax.broadcasted_iota(jnp.int32, (c_len, GLA_DK), 0)
    r2 = lax.broadcasted_iota(jnp.int32, (c_len, c_len), 0)
    c2 = lax.broadcasted_iota(jnp.int32, (c_len, c_len), 1)
    tril = (r2 >= c2).astype(F32)
    half = c_len // 2
    quarter = c_len // 4
    lvl2_mask = ((r2 // half) == (c2 // half)) & ((r2 // quarter) % 2 == 1) & ((c2 // quarter) % 2 == 0)
    diag_mask = (r2 // sub) == (c2 // sub)
    row_in_sub = row % sub

    def chunk(c, carry):
        r0 = pl.multiple_of(c * c_len, c_len)
        q = q_ref[pl.ds(r0, c_len), :].astype(F32) * (GLA_DK ** -0.5)
        k = k_ref[pl.ds(r0, c_len), :].astype(F32)
        v = v_ref[pl.ds(r0, c_len), :]
        la = la_ref[pl.ds(r0, c_len), :]
        b = jnp.dot(tril, la, preferred_element_type=F32, precision=HIGHEST)
        b_last = b[c_len - 1:c_len, :]

        ref1 = b[half - 1:half, :]
        q1 = jnp.where(row >= half, q * jnp.exp(jnp.minimum(b - ref1, 0.0)), 0.0)
        k1 = jnp.where(row < half, k * jnp.exp(jnp.minimum(ref1 - b, 0.0)), 0.0)
        scores = _dot_nt(q1.astype(BF16), k1.astype(BF16))
        ref2 = jnp.where(row < half, b[quarter - 1:quarter, :], b[half + quarter - 1:half + quarter, :])
        odd_q = (row // quarter) % 2 == 1
        q2 = jnp.where(odd_q, q * jnp.exp(jnp.minimum(b - ref2, 0.0)), 0.0)
        k2 = jnp.where(odd_q, 0.0, k * jnp.exp(jnp.minimum(ref2 - b, 0.0)))
        scores += jnp.where(lvl2_mask, _dot_nt(q2.astype(BF16), k2.astype(BF16)), 0.0)
        slabs = []
        for j in range(sub):
            kj = jnp.concatenate(
                [jnp.broadcast_to(k[s * sub + j:s * sub + j + 1, :], (sub, GLA_DK)) for s in range(n_sub)], axis=0)
            bj = jnp.concatenate(
                [jnp.broadcast_to(b[s * sub + j:s * sub + j + 1, :], (sub, GLA_DK)) for s in range(n_sub)], axis=0)
            z = q * kj * jnp.exp(jnp.minimum(b - bj, 0.0))
            slabs.append(jnp.where(row_in_sub >= j, z, 0.0).astype(BF16))
        zflat = jnp.concatenate(slabs, axis=1)
        sdiag = _dot(zflat, wsel_ref[...])[:, :c_len]
        scores += jnp.where(diag_mask, sdiag, 0.0)

        st = st_ref[...]
        o = _dot_nt((q * jnp.exp(b)).astype(BF16), st.astype(BF16))
        o += _dot(scores.astype(BF16), v)
        k_dec = (k * jnp.exp(b_last - b)).astype(BF16)
        st_ref[...] = st * jnp.exp(b_last) + _dot_tn(v, k_dec)

        g = g_ref[pl.ds(r0, c_len), :].astype(F32)
        o_ref[pl.ds(r0, c_len), :] = (_rms(o, nw_ref[...]) * _silu(g)).astype(o_ref.dtype)
        return carry

    lax.fori_loop(0, n_chunks, chunk, 0)


def _gla(q, k, v, g, sm, w2, b2, nw, wsel, *, batch, seq, tt):
    n = q.shape[0]
    tiles = seq // tt
    tokh = lambda w: pl.BlockSpec((tt, w), lambda b, h, t: (b * tiles + t, h))
    return pl.pallas_call(
        functools.partial(_gla_kernel, n_chunks=tt // GLA_CHUNK),
        grid=(batch, GLA_HEADS, tiles),
        in_specs=[
            tokh(GLA_DK), tokh(GLA_DK), tokh(GLA_DV), tokh(GLA_DV),
            pl.BlockSpec((tt, LANES), lambda b, h, t: (b * tiles + t, 0)),
            pl.BlockSpec((1, LANES, GLA_DK), lambda b, h, t: (h, 0, 0)),
            pl.BlockSpec((1, 1, GLA_DK), lambda b, h, t: (h, 0, 0)),
            pl.BlockSpec((1, GLA_DV), lambda b, h, t: (0, 0)),
            pl.BlockSpec(wsel.shape, lambda b, h, t: (0, 0)),
        ],
        out_specs=tokh(GLA_DV),
        out_shape=jax.ShapeDtypeStruct((n, GLA_HEADS * GLA_DV), BF16),
        scratch_shapes=[pltpu.VMEM((GLA_DV, GLA_DK), F32), pltpu.VMEM((tt, GLA_DK), F32)],
        compiler_params=_cparams("parallel", "parallel", "arbitrary"),
        name="gla",
    )(q, k, v, g, sm, w2, b2, nw, wsel)


def _ssd_kernel(xbc_ref, z_ref, sm_ref, dtt_ref, cw_ref, cb_ref, hexp_ref, dtb_ref, aneg_ref, dsk_ref,
                dtbc_ref, anegc_ref, nw_ref, y_ref, xpad_ref, u_ref, st_ref, *, n_chunks, tt):
    c_len = SSM_CHUNK
    halo = 8
    t_idx = pl.program_id(1)

    @pl.when(t_idx == 0)
    def _():
        st_ref[...] = jnp.zeros_like(st_ref)
        xpad_ref[0:halo, :] = jnp.zeros((halo, xpad_ref.shape[1]), F32)

    @pl.when(t_idx > 0)
    def _():
        xpad_ref[0:halo, :] = xpad_ref[tt:tt + halo, :]

    xpad_ref[halo:halo + tt, :] = xbc_ref[...].astype(F32)
    u = cb_ref[...] + cw_ref[SSM_CONV - 1:SSM_CONV, :] * xpad_ref[halo:halo + tt, :]
    for tap in range(SSM_CONV - 1):
        lo = halo - (SSM_CONV - 1) + tap
        u += cw_ref[tap:tap + 1, :] * xpad_ref[lo:lo + tt, :]
    u_ref[...] = _silu(u)

    r2 = lax.broadcasted_iota(jnp.int32, (c_len, c_len), 0)
    c2 = lax.broadcasted_iota(jnp.int32, (c_len, c_len), 1)
    causal = r2 >= c2
    tril = causal.astype(F32)
    triu = (r2 <= c2).astype(F32)

    def chunk(c, carry):
        r0 = pl.multiple_of(c * c_len, c_len)
        xs = u_ref[pl.ds(r0, c_len), 0:SSM_DINNER]
        bm = u_ref[pl.ds(r0, c_len), SSM_DINNER:SSM_DINNER + SSM_BC].astype(BF16)
        cm = u_ref[pl.ds(r0, c_len), SSM_DINNER + SSM_BC:SSM_DINNER + 2 * SSM_BC].astype(BF16)

        dt_raw = jnp.dot(sm_ref[pl.ds(r0, c_len), :], hexp_ref[...], preferred_element_type=F32,
                         precision=HIGHEST)
        dt = _softplus(dt_raw + dtb_ref[...])
        cs = jnp.dot(tril, dt * aneg_ref[...], preferred_element_type=F32, precision=HIGHEST)
        cs_last = cs[c_len - 1:c_len, :]
        dt_t = _softplus(dtt_ref[0, c] + dtbc_ref[...])
        cs_t = jnp.dot(dt_t * anegc_ref[...], triu, preferred_element_type=F32, precision=HIGHEST)

        xdt = xs * dt
        xdt_b = xdt.astype(BF16)
        xdec = (xdt * jnp.exp(cs_last - cs)).astype(BF16)
        ys = []
        for gi in range(SSM_GROUPS):
            gsl = slice(gi * SSM_DSTATE, (gi + 1) * SSM_DSTATE)
            wsl = slice(gi * SSM_GROUP_WIDTH, (gi + 1) * SSM_GROUP_WIDTH)
            cg, bg = cm[:, gsl], bm[:, gsl]
            cb = _dot_nt(cg, bg)
            st = st_ref[gi]
            y_off = _dot(cg, st.astype(BF16)) * jnp.exp(cs[:, wsl])
            y_diag = []
            for hi in range(SSM_HEADS_PER_GROUP):
                h = gi * SSM_HEADS_PER_GROUP + hi
                hsl = slice(h * SSM_HEADDIM, (h + 1) * SSM_HEADDIM)
                seg = cs[:, hsl] - cs_t[h:h + 1, :]
                lmat = jnp.exp(jnp.where(causal, seg, -jnp.inf))
                y_diag.append(_dot((cb * lmat).astype(BF16), xdt_b[:, hsl]))
            ys.append(y_off + jnp.concatenate(y_diag, axis=1))
            st_ref[gi] = st * jnp.exp(cs_last[:, wsl]) + _dot_tn(bg, xdec[:, wsl])
        y = jnp.concatenate(ys, axis=1) + dsk_ref[...] * xs
        y = y * _silu(z_ref[pl.ds(r0, c_len), :].astype(F32))
        outs = []
        for gi in range(SSM_GROUPS):
            wsl = slice(gi * SSM_GROUP_WIDTH, (gi + 1) * SSM_GROUP_WIDTH)
            outs.append(_rms(y[:, wsl], nw_ref[:, wsl]))
        y_ref[pl.ds(r0, c_len), :] = jnp.concatenate(outs, axis=1).astype(y_ref.dtype)
        return carry

    lax.fori_loop(0, n_chunks, chunk, 0)


def _ssd(xbc, z, sm, dtt, cw, cb, hexp, dtb, aneg, dsk, dtbc, anegc, nw, *, batch, seq, tt):
    n, cdim = xbc.shape
    tiles = seq // tt
    n_chunks = tt // SSM_CHUNK
    tok = lambda w: pl.BlockSpec((tt, w), lambda b, t: (b * tiles + t, 0))
    full = lambda a: pl.BlockSpec(a.shape, lambda b, t: (0,) * a.ndim)
    return pl.pallas_call(
        functools.partial(_ssd_kernel, n_chunks=n_chunks, tt=tt),
        grid=(batch, tiles),
        in_specs=[tok(cdim), tok(SSM_DINNER), tok(LANES),
                  pl.BlockSpec((1, n_chunks, SSM_HEADS, SSM_CHUNK), lambda b, t: (b, t, 0, 0)),
                  full(cw), full(cb), full(hexp), full(dtb), full(aneg), full(dsk), full(dtbc), full(anegc),
                  full(nw)],
        out_specs=tok(SSM_DINNER),
        out_shape=jax.ShapeDtypeStruct((n, SSM_DINNER), BF16),
        scratch_shapes=[pltpu.VMEM((tt + 8, cdim), F32), pltpu.VMEM((tt, cdim), F32),
                        pltpu.VMEM((SSM_GROUPS, SSM_DSTATE, SSM_GROUP_WIDTH), F32)],
        compiler_params=_cparams("parallel", "arbitrary"),
        name="ssd",
    )(xbc, z, sm, dtt, cw, cb, hexp, dtb, aneg, dsk, dtbc, anegc, nw)


def _mla_in_kernel(x_ref, pos_ref, pre_ref, wd_ref, qn_ref, wuq_ref, kvn_ref, wukv_ref, freq_ref,
                   q_ref, k_ref, v_ref):
    h = _rms(x_ref[...], pre_ref[...]).astype(BF16)
    d = _dot(h, wd_ref[...])
    cq = _rms(d[:, :MLA_Q_RANK], qn_ref[...]).astype(BF16)
    ckv = _rms(d[:, MLA_Q_RANK:MLA_Q_RANK + MLA_KV_RANK], kvn_ref[...]).astype(BF16)
    kr = d[:, MLA_Q_RANK + MLA_KV_RANK:]

    ang = pos_ref[...] * freq_ref[...]
    lane = lax.broadcasted_iota(jnp.int32, ang.shape, 1)
    cosv = jnp.where(lane < MLA_ROPE, jnp.cos(ang), 0.0)
    sinv = jnp.where(lane < MLA_ROPE, jnp.sin(ang), 0.0)

    def rope(xr):
        return xr * cosv + pltpu.roll(xr, MLA_ROPE, axis=1) * sinv

    k_rope = rope(kr).astype(BF16)
    q = _dot(cq, wuq_ref[...]) * (MLA_QK ** -0.5)
    kv = _dot(ckv, wukv_ref[...])
    for hd in range(MLA_HEADS):
        base = hd * MLA_HEAD_PAD
        q_ref[:, base:base + MLA_NOPE] = q[:, base:base + MLA_NOPE].astype(BF16)
        q_ref[:, base + MLA_NOPE:base + MLA_HEAD_PAD] = rope(q[:, base + MLA_NOPE:base + MLA_HEAD_PAD]).astype(BF16)
        k_ref[:, base:base + MLA_NOPE] = kv[:, hd * MLA_NOPE:(hd + 1) * MLA_NOPE].astype(BF16)
        k_ref[:, base + MLA_NOPE:base + MLA_HEAD_PAD] = k_rope
    v_ref[...] = kv[:, MLA_HEADS * MLA_NOPE:].astype(BF16)


def _mla_in(x, pos, pre, wd, qn, wuq, kvn, wukv, freq, *, tm):
    n, d = x.shape
    tok = lambda w: pl.BlockSpec((tm, w), lambda i: (i, 0))
    hp = MLA_HEADS * MLA_HEAD_PAD
    return pl.pallas_call(
        _mla_in_kernel,
        grid=(n // tm,),
        in_specs=[tok(d), tok(1), _const_spec((1, d)), _const_spec(wd.shape), _const_spec(qn.shape),
                  _const_spec(wuq.shape), _const_spec(kvn.shape), _const_spec(wukv.shape),
                  _const_spec(freq.shape)],
        out_specs=[tok(hp), tok(hp), tok(MLA_HEADS * MLA_V)],
        out_shape=[jax.ShapeDtypeStruct((n, hp), BF16), jax.ShapeDtypeStruct((n, hp), BF16),
                   jax.ShapeDtypeStruct((n, MLA_HEADS * MLA_V), BF16)],
        compiler_params=_cparams("parallel"),
        name="mla_in",
    )(x, pos, pre, wd, qn, wuq, kvn, wukv, freq)


def _attn_kernel(q_ref, k_ref, v_ref, o_ref, m_ref, l_ref, acc_ref, *, bq, bk):
    qi = pl.program_id(2)
    q = q_ref[...]
    m_ref[...] = jnp.full_like(m_ref, -jnp.inf)
    l_ref[...] = jnp.zeros_like(l_ref)
    acc_ref[...] = jnp.zeros_like(acc_ref)

    def step(k0, masked):
        kb = k_ref[pl.ds(k0, bk), :]
        vb = v_ref[pl.ds(k0, bk), :]
        s = _dot_nt(q, kb)
        if masked:
            rows = qi * bq + lax.broadcasted_iota(jnp.int32, (bq, bk), 0)
            cols = k0 + lax.broadcasted_iota(jnp.int32, (bq, bk), 1)
            s = jnp.where(rows >= cols, s, -jnp.inf)
        m_prev = m_ref[...]
        m_new = jnp.maximum(m_prev, jnp.max(s, axis=-1, keepdims=True))
        alpha = jnp.exp(m_prev - m_new)
        p = jnp.exp(s - m_new)
        l_ref[...] = alpha * l_ref[...] + jnp.sum(p, axis=-1, keepdims=True)
        acc_ref[...] = alpha * acc_ref[...] + _dot(p.astype(BF16), vb)
        m_ref[...] = m_new

    n_full = (qi * bq) // bk

    def full_body(j, carry):
        step(pl.multiple_of(j * bk, bk), masked=False)
        return carry

    lax.fori_loop(0, n_full, full_body, 0)
    for d in range(bq // bk):
        step(pl.multiple_of(qi * bq + d * bk, bk), masked=True)
    o_ref[...] = (acc_ref[...] / l_ref[...]).astype(o_ref.dtype)


def _attn(q, k, v, *, batch, seq, bq, bk):
    n = q.shape[0]
    nq = seq // bq
    return pl.pallas_call(
        functools.partial(_attn_kernel, bq=bq, bk=bk),
        grid=(batch, MLA_HEADS, nq),
        in_specs=[
            pl.BlockSpec((bq, MLA_HEAD_PAD), lambda b, h, i: (b * nq + i, h)),
            pl.BlockSpec((seq, MLA_HEAD_PAD), lambda b, h, i: (b, h)),
            pl.BlockSpec((seq, MLA_V), lambda b, h, i: (b, h)),
        ],
        out_specs=pl.BlockSpec((bq, MLA_V), lambda b, h, i: (b * nq + i, h)),
        out_shape=jax.ShapeDtypeStruct((n, MLA_HEADS * MLA_V), BF16),
        scratch_shapes=[pltpu.VMEM((bq, 1), F32), pltpu.VMEM((bq, 1), F32), pltpu.VMEM((bq, MLA_V), F32)],
        compiler_params=_cparams("parallel", "parallel", "arbitrary"),
        name="mla_attn",
    )(q, k, v)


def _prep_even(ab_w_in, gla_w_gate2, gla_b_gate2, gla_norm, conv_w, conv_b, dt_bias, a_log, d_skip, ssm_norm,
               ab_w_out):
    pts = [0, 512, 1024, 2048, 3072, 3088, 4112, 6160, 6176]
    wq, wk, wv, wg, wa, wz, wx, wdt = (ab_w_in[:, pts[i]:pts[i + 1]] for i in range(8))
    wbig = jnp.concatenate([wq, wk, wv, wg, wz, wx], axis=1).astype(BF16)
    d = ab_w_in.shape[0]
    wsm = jnp.zeros((d, LANES), F32).at[:, :GLA_GATE_RANK].set(wa)
    wsm = wsm.at[:, GLA_GATE_RANK:GLA_GATE_RANK + SSM_HEADS].set(wdt).astype(BF16)
    w2 = jnp.zeros((LANES, GLA_HEADS * GLA_DK), F32).at[:GLA_GATE_RANK].set(gla_w_gate2)
    w2 = w2.reshape(LANES, GLA_HEADS, GLA_DK).transpose(1, 0, 2)
    b2 = gla_b_gate2.reshape(GLA_HEADS, 1, GLA_DK)
    jj = jnp.arange(GLA_SUB * GLA_DK) // GLA_DK
    wsel = (jj[:, None] == (jnp.arange(LANES)[None, :] % GLA_SUB)).astype(BF16)
    lanes = jnp.arange(LANES)[:, None]
    heads = jnp.arange(SSM_DINNER)[None, :] // SSM_HEADDIM
    hexp = (lanes == heads + GLA_GATE_RANK).astype(F32)
    rep = lambda p: jnp.repeat(p.astype(F32), SSM_HEADDIM)[None, :]
    a_neg = -jnp.exp(a_log.astype(F32))
    return dict(
        wbig=wbig, wsm=wsm, w2=w2, b2=b2, gla_nw=gla_norm[None, :], wsel=wsel,
        cw=conv_w, cb=conv_b[None, :], hexp=hexp, dtb=rep(dt_bias), aneg=rep(a_neg), dsk=rep(d_skip),
        dtbc=dt_bias[:, None], anegc=a_neg[:, None], ssm_nw=ssm_norm[None, :],
        wo_gla=ab_w_out[:GLA_HEADS * GLA_DV].astype(BF16), wo_ssd=ab_w_out[GLA_HEADS * GLA_DV:].astype(BF16))


def _prep_odd(w_down, q_norm, w_uq, kv_norm, w_ukv, w_o):
    half = MLA_ROPE // 2

    def with_rot(w_rope):
        return jnp.concatenate([w_rope, -w_rope[..., half:], w_rope[..., :half]], axis=-1)

    lat = MLA_Q_RANK + MLA_KV_RANK
    wd = jnp.concatenate([w_down[:, :lat], with_rot(w_down[:, lat:])], axis=1).astype(BF16)
    uq = w_uq.reshape(MLA_Q_RANK, MLA_HEADS, MLA_QK)
    wuq = jnp.concatenate([uq[..., :MLA_NOPE], with_rot(uq[..., MLA_NOPE:])], axis=-1)
    wuq = wuq.reshape(MLA_Q_RANK, MLA_HEADS * MLA_HEAD_PAD).astype(BF16)
    ukv = w_ukv.reshape(MLA_KV_RANK, MLA_HEADS, MLA_NOPE + MLA_V)
    wukv = jnp.concatenate([ukv[..., :MLA_NOPE].reshape(MLA_KV_RANK, -1),
                            ukv[..., MLA_NOPE:].reshape(MLA_KV_RANK, -1)], axis=1).astype(BF16)
    inv_freq = ROPE_THETA ** (-jnp.arange(0, MLA_ROPE, 2, dtype=F32) / MLA_ROPE)
    freq = jnp.tile(inv_freq, LANES // half)[None, :]
    return dict(wd=wd, qn=q_norm[None, :], wuq=wuq, kvn=kv_norm[None, :], wukv=wukv, freq=freq,
                wo=w_o.astype(BF16))


FFN_CHUNK = 256
TOKEN_TILE = 512
SEQ_TILE = 512
ATTN_BQ = 512
ATTN_BK = 512


def kernel(x, positions, ffn1_pre_norm, ffn1_w_gate, ffn1_w_up, ffn1_w_down, ffn1_post_norm, mix_pre_norm, mix_post_norm, ffn2_pre_norm, ffn2_w_gate, ffn2_w_up, ffn2_w_down, ffn2_post_norm, ab_w_in, gla_w_gate2, gla_b_gate2, gla_norm, ssm_conv_w, ssm_conv_b, ssm_dt_bias, ssm_a_log, ssm_d, ssm_norm, ab_w_out, mla_w_down, mla_q_norm, mla_w_uq, mla_kv_norm, mla_w_ukv, mla_w_o):
    batch, seq, d = x.shape
    depth = ffn1_pre_norm.shape[0]
    n = batch * seq
    tm = min(TOKEN_TILE, seq)
    tt = min(SEQ_TILE, seq)
    xf = x.reshape(n, d)
    pos = positions.astype(F32).reshape(n, 1)

    for i in range(depth):
        wgu, wdn = _prep_ffn(ffn1_w_gate[i], ffn1_w_up[i], ffn1_w_down[i], FFN_CHUNK)
        xf = _ffn(xf, ffn1_pre_norm[i][None, :], wgu, wdn, ffn1_post_norm[i][None, :], tm=tm)
        j = i // 2
        if i % 2 == 0:
            p = _prep_even(ab_w_in[j], gla_w_gate2[j], gla_b_gate2[j], gla_norm[j], ssm_conv_w[j], ssm_conv_b[j],
                           ssm_dt_bias[j], ssm_a_log[j], ssm_d[j], ssm_norm[j], ab_w_out[j])
            q, k, v, g, z, xbc, sm = _ab_in(xf, mix_pre_norm[i][None, :], p["wbig"], p["wsm"], tm=tm)
            dtt = sm[:, GLA_GATE_RANK:GLA_GATE_RANK + SSM_HEADS].reshape(batch, seq // SSM_CHUNK, SSM_CHUNK, SSM_HEADS)
            dtt = dtt.transpose(0, 1, 3, 2)
            o_gla = _gla(q, k, v, g, sm, p["w2"], p["b2"], p["gla_nw"], p["wsel"], batch=batch, seq=seq, tt=tt)
            y_ssd = _ssd(xbc, z, sm, dtt, p["cw"], p["cb"], p["hexp"], p["dtb"], p["aneg"], p["dsk"], p["dtbc"],
                         p["anegc"], p["ssm_nw"], batch=batch, seq=seq, tt=tt)
            xf = _mix_out(xf, [o_gla, y_ssd], [p["wo_gla"], p["wo_ssd"]], mix_post_norm[i][None, :], tm=tm)
        else:
            p = _prep_odd(mla_w_down[j], mla_q_norm[j], mla_w_uq[j], mla_kv_norm[j], mla_w_ukv[j], mla_w_o[j])
            qf, kf, vf = _mla_in(xf, pos, mix_pre_norm[i][None, :], p["wd"], p["qn"], p["wuq"], p["kvn"],
                                 p["wukv"], p["freq"], tm=tm)
            o = _attn(qf, kf, vf, batch=batch, seq=seq, bq=min(ATTN_BQ, seq), bk=min(ATTN_BK, seq))
            xf = _mix_out(xf, [o], [p["wo"]], mix_post_norm[i][None, :], tm=tm)
        wgu, wdn = _prep_ffn(ffn2_w_gate[i], ffn2_w_up[i], ffn2_w_down[i], FFN_CHUNK)
        xf = _ffn(xf, ffn2_pre_norm[i][None, :], wgu, wdn, ffn2_post_norm[i][None, :], tm=tm)
    return xf.reshape(batch, seq, d)
```
